```python
import math
import jax, jax.numpy as jnp
from jax import lax
import numpy as np

D_MODEL = 2048
BATCH = 4
SEQ = 4096
DEPTH = 4

N_DIFF_HEADS = 8
DIFF_HEAD_DIM = 64
DIFF_V_DIM = 2 * DIFF_HEAD_DIM
ATTN_WIDTH = N_DIFF_HEADS * DIFF_V_DIM
CONV_WIDTH = D_MODEL - ATTN_WIDTH
CONV_K = 3
IN_PROJ_WIDTH = 3 * ATTN_WIDTH + 3 * CONV_WIDTH
D_FF = -(-8 * D_MODEL // (3 * 256)) * 256
ROPE_THETA = 10000.0
EPS = 1e-6
Q_BLOCK = 128

kernel_name = 'hybrid_diffattn_shortconv_encoder'


def rmsnorm(t, g):
    tf = t.astype(jnp.float32)
    tf = tf * lax.rsqrt(jnp.mean(tf * tf, axis=-1, keepdims=True) + EPS)
    return (tf * g.astype(jnp.float32)).astype(t.dtype)


def rope_tables(seq):
    pos = jnp.arange(seq, dtype=jnp.float32)
    inv_freq = ROPE_THETA ** (-jnp.arange(0, DIFF_HEAD_DIM, 2, dtype=jnp.float32) / DIFF_HEAD_DIM)
    ang = pos[:, None] * inv_freq[None, :]
    return jnp.cos(ang)[:, None, :], jnp.sin(ang)[:, None, :]


def apply_rope(t, cos, sin):
    tf = t.astype(jnp.float32)
    t1, t2 = jnp.split(tf, 2, axis=-1)
    return jnp.concatenate([t1 * cos - t2 * sin, t2 * cos + t1 * sin], axis=-1).astype(t.dtype)


def diff_attention(q, k, v, lam):
    b, s = q.shape[0], q.shape[1]
    nb = s // Q_BLOCK
    scale = DIFF_HEAD_DIM ** -0.5
    qf = (q.astype(jnp.float32) * scale).reshape(b, nb, Q_BLOCK, 2 * N_DIFF_HEADS, DIFF_HEAD_DIM)
    qf = qf.transpose(1, 0, 2, 3, 4)
    kf = k.astype(jnp.float32)

    def one_block(qb):
        sc = jnp.einsum('bqhd,bkhd->bhqk', qb, kf)
        p = jax.nn.softmax(sc, axis=-1).reshape(b, N_DIFF_HEADS, 2, Q_BLOCK, s)
        w = p[:, :, 0] - lam * p[:, :, 1]
        return jnp.einsum('bhqk,bkhe->bqhe', w.astype(v.dtype), v)

    o = lax.map(one_block, qf)
    return o.transpose(1, 0, 2, 3, 4).reshape(b, s, N_DIFF_HEADS, DIFF_V_DIM)


def short_gated_conv(gate_b, gate_c, u, w):
    z = gate_c * u
    zp = jnp.pad(z, ((0, 0), (1, 1), (0, 0)))
    y = zp[:, :-2] * w[0] + zp[:, 1:-1] * w[1] + zp[:, 2:] * w[2]
    return gate_b * y


def setup_inputs(seed: int = 0) -> dict:
    key = jax.random.key(seed)
    ks = jax.random.split(key, 15)
    f32 = jnp.float32
    nrm = lambda k, shape, s: jax.random.normal(k, shape, f32) * s
    return {
        'x': nrm(ks[0], (BATCH, SEQ, D_MODEL), 1.0),
        'norm_mix': 1.0 + nrm(ks[1], (DEPTH, D_MODEL), 0.02),
        'w_in': nrm(ks[2], (DEPTH, D_MODEL, IN_PROJ_WIDTH), D_MODEL ** -0.5),
        'lambda_q1': nrm(ks[3], (DEPTH, DIFF_HEAD_DIM), 0.1),
        'lambda_k1': nrm(ks[4], (DEPTH, DIFF_HEAD_DIM), 0.1),
        'lambda_q2': nrm(ks[5], (DEPTH, DIFF_HEAD_DIM), 0.1),
        'lambda_k2': nrm(ks[6], (DEPTH, DIFF_HEAD_DIM), 0.1),
        'subln': 1.0 + nrm(ks[7], (DEPTH, DIFF_V_DIM), 0.02),
        'conv_w': nrm(ks[8], (DEPTH, CONV_K, CONV_WIDTH), CONV_K ** -0.5),
        'w_out': nrm(ks[9], (DEPTH, D_MODEL, D_MODEL), D_MODEL ** -0.5),
        'norm_ffn': 1.0 + nrm(ks[10], (DEPTH, D_MODEL), 0.02),
        'w_gate': nrm(ks[11], (DEPTH, D_MODEL, D_FF), D_MODEL ** -0.5),
        'w_up': nrm(ks[12], (DEPTH, D_MODEL, D_FF), D_MODEL ** -0.5),
        'w_down': nrm(ks[13], (DEPTH, D_FF, D_MODEL), D_FF ** -0.5),
        'norm_final': 1.0 + nrm(ks[14], (D_MODEL,), 0.02),
    }


def reference(x, norm_mix, w_in, lambda_q1, lambda_k1, lambda_q2, lambda_k2, subln,
              conv_w, w_out, norm_ffn, w_gate, w_up, w_down, norm_final):
    b, s, _ = x.shape
    cos, sin = rope_tables(s)
    splits = [ATTN_WIDTH, 2 * ATTN_WIDTH, 3 * ATTN_WIDTH,
              3 * ATTN_WIDTH + CONV_WIDTH, 3 * ATTN_WIDTH + 2 * CONV_WIDTH]
    for l in range(DEPTH):
        lam_init = 0.8 - 0.6 * math.exp(-0.3 * l)
        h = rmsnorm(x, norm_mix[l])
        proj = h @ w_in[l]
        q, k, v, gate_b, gate_c, u = jnp.split(proj, splits, axis=-1)
        q = apply_rope(q.reshape(b, s, 2 * N_DIFF_HEADS, DIFF_HEAD_DIM), cos, sin)
        k = apply_rope(k.reshape(b, s, 2 * N_DIFF_HEADS, DIFF_HEAD_DIM), cos, sin)
        v = v.reshape(b, s, N_DIFF_HEADS, DIFF_V_DIM)
        lam = (jnp.exp(jnp.sum(lambda_q1[l].astype(jnp.float32) * lambda_k1[l].astype(jnp.float32)))
               - jnp.exp(jnp.sum(lambda_q2[l].astype(jnp.float32) * lambda_k2[l].astype(jnp.float32)))
               + lam_init)
        o_attn = diff_attention(q, k, v, lam)
        o_attn = (rmsnorm(o_attn, subln[l]) * (1.0 - lam_init)).reshape(b, s, ATTN_WIDTH)
        o_conv = short_gated_conv(gate_b, gate_c, u, conv_w[l])
        x = x + jnp.concatenate([o_attn.astype(x.dtype), o_conv.astype(x.dtype)], axis=-1) @ w_out[l]
        h = rmsnorm(x, norm_ffn[l])
        x = x + (jax.nn.silu(h @ w_gate[l]) * (h @ w_up[l])) @ w_down[l]
    return rmsnorm(x, norm_final)
```

```python
import functools
import math

import jax
import jax.numpy as jnp
from jax import lax
from jax.experimental import pallas as pl
from jax.experimental.pallas import tpu as pltpu

N_HEADS = 8
HEAD_DIM = 64
V_DIM = 2 * HEAD_DIM
ROPE_THETA = 10000.0
EPS = 1e-6
LANES = 128

F32 = jnp.float32
BF16 = jnp.bfloat16

VMEM_LIMIT = 56 * 1024 * 1024


def _rms(x, g):
    ms = jnp.mean(x * x, axis=-1, keepdims=True)
    return x * lax.rsqrt(ms + EPS) * g


def _inproj_kernel(x_ref, g_ref, w_ref, rope_ref, o_ref, h_ref, *, n_q_tiles, n_rope_tiles, q_scale):
    j = pl.program_id(1)

    @pl.when(j == 0)
    def _():
        h_ref[...] = _rms(x_ref[...], g_ref[...]).astype(BF16)

    acc = jnp.dot(h_ref[...], w_ref[...], preferred_element_type=F32)

    @pl.when(j < n_rope_tiles)
    def _():
        cos = rope_ref[0]
        sin_lo = rope_ref[1]
        sin_hi = rope_ref[2]
        scale = jnp.where(j < n_q_tiles, q_scale, 1.0).astype(F32)
        for c in range(acc.shape[1] // LANES):
            a = acc[:, c * LANES:(c + 1) * LANES]
            r = (a * cos
                 + pltpu.roll(a, LANES - HEAD_DIM // 2, axis=1) * sin_lo
                 + pltpu.roll(a, HEAD_DIM // 2, axis=1) * sin_hi)
            o_ref[:, c * LANES:(c + 1) * LANES] = (r * scale).astype(o_ref.dtype)

    @pl.when(j >= n_rope_tiles)
    def _():
        o_ref[...] = acc.astype(o_ref.dtype)


def _inproj(x, norm_w, w_in, rope, layer, *, seq, attn_width, tm=1024, tn=1024):
    m, d = x.shape
    n = w_in.shape[2]
    kern = functools.partial(
        _inproj_kernel, n_q_tiles=attn_width // tn, n_rope_tiles=2 * attn_width // tn,
        q_scale=HEAD_DIM ** -0.5)
    pos_blocks = seq // tm
    return pl.pallas_call(
        kern,
        grid=(m // tm, n // tn),
        in_specs=[
            pl.BlockSpec((tm, d), lambda i, j: (i, 0)),
            pl.BlockSpec((None, 1, d), lambda i, j: (layer, 0, 0)),
            pl.BlockSpec((None, d, tn), lambda i, j: (layer, 0, j)),
            pl.BlockSpec((3, tm, LANES), lambda i, j: (0, i % pos_blocks, 0)),
        ],
        out_specs=pl.BlockSpec((tm, tn), lambda i, j: (i, j)),
        out_shape=jax.ShapeDtypeStruct((m, n), BF16),
        scratch_shapes=[pltpu.VMEM((tm, d), BF16)],
        compiler_params=pltpu.CompilerParams(
            dimension_semantics=("arbitrary", "arbitrary"), vmem_limit_bytes=VMEM_LIMIT),
        name="inproj",
    )(x, norm_w, w_in, rope)


def _attn_kernel(q_ref, k_ref, v_ref, lam_ref, g_ref, o_ref, vt_ref, qd_ref, *, lam_init, tq, tk):
    seq = k_ref.shape[0]
    lp = lam_ref[...]
    lam = (jnp.exp(jnp.sum(lp[0:1] * lp[1:2], axis=-1, keepdims=True))
           - jnp.exp(jnp.sum(lp[2:3] * lp[3:4], axis=-1, keepdims=True)) + lam_init)

    for c in range(seq // tk):
        vt_ref[:, c * tk:(c + 1) * tk] = v_ref[c * tk:(c + 1) * tk, :].astype(F32).T.astype(BF16)
    qd_ref[...] = jnp.zeros_like(qd_ref)

    def q_tile(qi, carry):
        q0 = pl.multiple_of(qi * tq, tq)
        qt = q_ref[pl.ds(q0, tq), :].astype(F32).T
        qd_ref[0:HEAD_DIM, 0:tq] = qt[0:HEAD_DIM].astype(BF16)
        qd_ref[HEAD_DIM:V_DIM, tq:2 * tq] = qt[HEAD_DIM:V_DIM].astype(BF16)
        qd = qd_ref[...]
        m = l = acc = None
        for c in range(seq // tk):
            s = jnp.dot(k_ref[c * tk:(c + 1) * tk, :], qd, preferred_element_type=F32)
            m_c = jnp.max(s, axis=0, keepdims=True)
            if c == 0:
                m = m_c
                p = jnp.exp(s - m)
                l = jnp.sum(p, axis=0, keepdims=True)
                acc = jnp.dot(vt_ref[:, 0:tk], p.astype(BF16), preferred_element_type=F32)
            else:
                m_new = jnp.maximum(m, m_c)
                alpha = jnp.exp(m - m_new)
                p = jnp.exp(s - m_new)
                l = alpha * l + jnp.sum(p, axis=0, keepdims=True)
                acc = alpha * acc + jnp.dot(vt_ref[:, c * tk:(c + 1) * tk], p.astype(BF16),
                                            preferred_element_type=F32)
                m = m_new
        inv = 1.0 / l
        o = acc[:, 0:tq] * inv[:, 0:tq] - lam * (acc[:, tq:2 * tq] * inv[:, tq:2 * tq])
        ms = jnp.mean(o * o, axis=0, keepdims=True)
        on = (o * lax.rsqrt(ms + EPS)).T
        o_ref[pl.ds(q0, tq), :] = (on * g_ref[...] * (1.0 - lam_init)).astype(o_ref.dtype)
        return carry

    lax.fori_loop(0, seq // tq, q_tile, 0)


def _attention(proj, lam_params, subln, layer, *, batch, seq, attn_width, lam_init, tq=256, tk=512):
    m = proj.shape[0]
    k_off = attn_width // V_DIM
    v_off = 2 * attn_width // V_DIM
    kern = functools.partial(_attn_kernel, lam_init=lam_init, tq=tq, tk=tk)
    return pl.pallas_call(
        kern,
        grid=(batch, N_HEADS),
        in_specs=[
            pl.BlockSpec((seq, V_DIM), lambda b, h: (b, h)),
            pl.BlockSpec((seq, V_DIM), lambda b, h: (b, k_off + h)),
            pl.BlockSpec((seq, V_DIM), lambda b, h: (b, v_off + h)),
            pl.BlockSpec((None, 4, HEAD_DIM), lambda b, h: (layer, 0, 0)),
            pl.BlockSpec((None, 1, V_DIM), lambda b, h: (layer, 0, 0)),
        ],
        out_specs=pl.BlockSpec((seq, V_DIM), lambda b, h: (b, h)),
        out_shape=jax.ShapeDtypeStruct((m, attn_width), BF16),
        scratch_shapes=[pltpu.VMEM((V_DIM, seq), BF16), pltpu.VMEM((V_DIM, 2 * tq), BF16)],
        compiler_params=pltpu.CompilerParams(
            dimension_semantics=("arbitrary", "arbitrary"), vmem_limit_bytes=VMEM_LIMIT),
        name="diffattn",
    )(proj, proj, proj, lam_params, subln)


def _conv_kernel(b_ref, c_ref, u_ref, w_ref, o_ref):
    seq = b_ref.shape[0]
    z = c_ref[...].astype(F32) * u_ref[...].astype(F32)
    row = lax.broadcasted_iota(jnp.int32, z.shape, 0)
    z_prev = jnp.where(row == 0, 0.0, pltpu.roll(z, 1, axis=0))
    z_next = jnp.where(row == seq - 1, 0.0, pltpu.roll(z, seq - 1, axis=0))
    w = w_ref[...]
    y = z_prev * w[0:1] + z * w[1:2] + z_next * w[2:3]
    o_ref[...] = (b_ref[...].astype(F32) * y).astype(o_ref.dtype)


def _short_conv(proj, conv_w, layer, *, batch, seq, attn_width, conv_width):
    m = proj.shape[0]
    nb = conv_width // LANES
    b_off = 3 * attn_width // LANES
    return pl.pallas_call(
        _conv_kernel,
        grid=(batch, nb),
        in_specs=[
            pl.BlockSpec((seq, LANES), lambda b, c: (b, b_off + c)),
            pl.BlockSpec((seq, LANES), lambda b, c: (b, b_off + nb + c)),
            pl.BlockSpec((seq, LANES), lambda b, c: (b, b_off + 2 * nb + c)),
            pl.BlockSpec((None, 3, LANES), lambda b, c: (layer, 0, c)),
        ],
        out_specs=pl.BlockSpec((seq, LANES), lambda b, c: (b, c)),
        out_shape=jax.ShapeDtypeStruct((m, conv_width), BF16),
        compiler_params=pltpu.CompilerParams(
            dimension_semantics=("arbitrary", "arbitrary"), vmem_limit_bytes=VMEM_LIMIT),
        name="shortconv",
    )(proj, proj, proj, conv_w)


def _outproj_kernel(oa_ref, oc_ref, wa_ref, wc_ref, x_ref, o_ref):
    acc = (jnp.dot(oa_ref[...], wa_ref[...], preferred_element_type=F32)
           + jnp.dot(oc_ref[...], wc_ref[...], preferred_element_type=F32))
    o_ref[...] = x_ref[...] + acc


def _outproj(o_attn, o_conv, w_out, x, layer, *, tm=1024, tn=1024):
    m, d = x.shape
    ka = o_attn.shape[1]
    kc = o_conv.shape[1]
    assert ka == kc
    return pl.pallas_call(
        _outproj_kernel,
        grid=(m // tm, d // tn),
        in_specs=[
            pl.BlockSpec((tm, ka), lambda i, j: (i, 0)),
            pl.BlockSpec((tm, kc), lambda i, j: (i, 0)),
            pl.BlockSpec((None, None, ka, tn), lambda i, j: (layer, 0, 0, j)),
            pl.BlockSpec((None, None, kc, tn), lambda i, j: (layer, 1, 0, j)),
            pl.BlockSpec((tm, tn), lambda i, j: (i, j)),
        ],
        out_specs=pl.BlockSpec((tm, tn), lambda i, j: (i, j)),
        out_shape=jax.ShapeDtypeStruct((m, d), F32),
        compiler_params=pltpu.CompilerParams(
            dimension_semantics=("arbitrary", "arbitrary"), vmem_limit_bytes=VMEM_LIMIT),
        name="outproj",
    )(o_attn, o_conv, w_out, w_out, x)


def _ffn_kernel(x_ref, g_ref, wg_ref, wu_ref, wd_ref, gf_ref, o_ref, h_ref, *, final_norm):
    f = pl.program_id(1)

    @pl.when(f == 0)
    def _():
        x = x_ref[...]
        h_ref[...] = _rms(x, g_ref[...]).astype(BF16)
        o_ref[...] = x

    h = h_ref[...]
    gate = jnp.dot(h, wg_ref[...], preferred_element_type=F32)
    up = jnp.dot(h, wu_ref[...], preferred_element_type=F32)
    act = gate * (1.0 / (1.0 + jnp.exp(-gate))) * up
    o_ref[...] += jnp.dot(act.astype(BF16), wd_ref[...], preferred_element_type=F32)

    if final_norm:
        @pl.when(f == pl.num_programs(1) - 1)
        def _():
            o_ref[...] = _rms(o_ref[...], gf_ref[...])


def _ffn(x, norm_w, w_gate, w_up, w_down, norm_final, layer, *, final_norm, tm=512, tf=512):
    m, d = x.shape
    dff = w_gate.shape[2]
    kern = functools.partial(_ffn_kernel, final_norm=final_norm)
    return pl.pallas_call(
        kern,
        grid=(m // tm, dff // tf),
        in_specs=[
            pl.BlockSpec((tm, d), lambda i, f: (i, 0)),
            pl.BlockSpec((None, 1, d), lambda i, f: (layer, 0, 0)),
            pl.BlockSpec((None, d, tf), lambda i, f: (layer, 0, f)),
            pl.BlockSpec((None, d, tf), lambda i, f: (layer, 0, f)),
            pl.BlockSpec((None, tf, d), lambda i, f: (layer, f, 0)),
            pl.BlockSpec((1, d), lambda i, f: (0, 0)),
        ],
        out_specs=pl.BlockSpec((tm, d), lambda i, f: (i, 0)),
        out_shape=jax.ShapeDtypeStruct((m, d), F32),
        scratch_shapes=[pltpu.VMEM((tm, d), BF16)],
        compiler_params=pltpu.CompilerParams(
            dimension_semantics=("arbitrary", "arbitrary"), vmem_limit_bytes=VMEM_LIMIT),
        name="ffn",
    )(x, norm_w, w_gate, w_up, w_down, norm_final)


def _rope_tables(seq):
    pos = jnp.arange(seq, dtype=F32)
    inv_freq = ROPE_THETA ** (-jnp.arange(0, HEAD_DIM, 2, dtype=F32) / HEAD_DIM)
    ang = pos[:, None] * inv_freq[None, :]
    cos, sin = jnp.cos(ang), jnp.sin(ang)
    zero = jnp.zeros_like(sin)
    reps = LANES // HEAD_DIM
    cos_t = jnp.tile(jnp.concatenate([cos, cos], axis=-1), (1, reps))
    sin_lo = jnp.tile(jnp.concatenate([-sin, zero], axis=-1), (1, reps))
    sin_hi = jnp.tile(jnp.concatenate([zero, sin], axis=-1), (1, reps))
    return jnp.stack([cos_t, sin_lo, sin_hi])


def kernel(x, norm_mix, w_in, lambda_q1, lambda_k1, lambda_q2, lambda_k2, subln, conv_w, w_out,
           norm_ffn, w_gate, w_up, w_down, norm_final):
    batch, seq, d = x.shape
    depth = w_in.shape[0]
    attn_width = N_HEADS * V_DIM
    conv_width = d - attn_width
    assert w_in.shape[2] == 3 * attn_width + 3 * conv_width

    rope = _rope_tables(seq)
    lam_params = jnp.stack([lambda_q1, lambda_k1, lambda_q2, lambda_k2], axis=1).astype(F32)
    w_in_b, w_gate_b, w_up_b, w_down_b = (w.astype(BF16) for w in (w_in, w_gate, w_up, w_down))
    w_out_b = w_out.astype(BF16).reshape(depth, 2, attn_width, d)
    norm_mix3 = norm_mix.reshape(depth, 1, d)
    norm_ffn3 = norm_ffn.reshape(depth, 1, d)
    subln3 = subln.reshape(depth, 1, V_DIM)
    norm_final2 = norm_final.reshape(1, d)

    xf = x.reshape(batch * seq, d)
    for layer in range(depth):
        lam_init = 0.8 - 0.6 * math.exp(-0.3 * layer)
        proj = _inproj(xf, norm_mix3, w_in_b, rope, layer, seq=seq, attn_width=attn_width)
        o_attn = _attention(proj, lam_params, subln3, layer, batch=batch, seq=seq,
                            attn_width=attn_width, lam_init=lam_init)
        o_conv = _short_conv(proj, conv_w, layer, batch=batch, seq=seq,
                             attn_width=attn_width, conv_width=conv_width)
        xf = _outproj(o_attn, o_conv, w_out_b, xf, layer)
        xf = _ffn(xf, norm_ffn3, w_gate_b, w_up_b, w_down_b, norm_final2, layer,
                  final_norm=(layer == depth - 1))
    return xf.reshape(batch, seq, d)
```
